```python
import functools
import jax
import jax.numpy as jnp
from jax import lax
import numpy as np

D_MODEL = 1024
BATCH = 8
SEQ = 2048
DEPTH = 1
DEC_BATCH = 128
DEC_SEQ = 8
PAST_LEN = 16384
PAGE_SIZE = 128

D_MIX = D_MODEL
MLA_HEADS = 8
D_NOPE = 64
D_ROPE = 32
D_V = 64
D_C = D_MODEL // 4
Q_LORA = D_MODEL // 2
ROPE_BASE = 10000.0
MLA_SCALE = (D_NOPE + D_ROPE) ** -0.5
Q_BLOCK = 128
D_RNN = D_MIX - MLA_HEADS * D_V
RNN_BLOCKS = 8
RNN_BLOCK = D_RNN // RNN_BLOCKS
RNN_CONV = 4
RGLRU_C = 8.0
N_MEM = 256
MEM_HEADS = 4
MEM_HEAD_DIM = D_MODEL // MEM_HEADS
MEM_SCALE = MEM_HEAD_DIM ** -0.5
D_FF = 11 * D_MODEL // 4
FFN_CONV = 3
ALPHA = (2.0 * DEPTH) ** 0.25
BETA = (8.0 * DEPTH) ** -0.25
LN_EPS = 1e-5
RMS_EPS = 1e-6
IN_SPLITS = (Q_LORA, Q_LORA + D_C, Q_LORA + D_C + D_ROPE, Q_LORA + D_C + D_ROPE + D_RNN)
D_IN = Q_LORA + D_C + D_ROPE + 2 * D_RNN

kernel_name = 'hybrid_mla_rglru_memxattn_convffn_step'


def layer_norm(x, g, b):
    xf = x.astype(jnp.float32)
    mu = jnp.mean(xf, axis=-1, keepdims=True)
    var = jnp.mean(jnp.square(xf - mu), axis=-1, keepdims=True)
    return ((xf - mu) * lax.rsqrt(var + LN_EPS) * g + b).astype(x.dtype)


def rms_norm(x, g):
    xf = x.astype(jnp.float32)
    return (xf * lax.rsqrt(jnp.mean(jnp.square(xf), axis=-1, keepdims=True) + RMS_EPS) * g).astype(x.dtype)


def rope(x, pos):
    half = D_ROPE // 2
    inv_freq = ROPE_BASE ** (-jnp.arange(half, dtype=jnp.float32) / half)
    ang = pos.astype(jnp.float32)[:, None] * inv_freq[None, :]
    shape = (1, pos.shape[0]) + (1,) * (x.ndim - 3) + (half,)
    cos = jnp.cos(ang).reshape(shape)
    sin = jnp.sin(ang).reshape(shape)
    xf = x.astype(jnp.float32)
    x1, x2 = xf[..., :half], xf[..., half:]
    return jnp.concatenate([x1 * cos - x2 * sin, x2 * cos + x1 * sin], axis=-1).astype(x.dtype)


def causal_dwconv(x, buf, w, b):
    width = w.shape[0]
    t = x.shape[1]
    xp = jnp.concatenate([buf.astype(x.dtype), x], axis=1)
    y = b + xp[:, 0:t] * w[0]
    for k in range(1, width):
        y = y + xp[:, k:k + t] * w[k]
    return y, xp[:, t:]


def mla_attend(q_lat, q_rope, segments, q_pos, k_pos):
    s = jnp.concatenate(
        [jnp.einsum('bqhc,bkc->bhqk', q_lat, c) + jnp.einsum('bqhr,bkr->bhqk', q_rope, kr)
         for c, kr in segments], axis=-1)
    s = s.astype(jnp.float32) * MLA_SCALE
    s = jnp.where(k_pos[None, None, None, :] <= q_pos[None, None, :, None], s, -1e30)
    p = jax.nn.softmax(s, axis=-1)
    out = None
    start = 0
    for c, _ in segments:
        n = c.shape[1]
        part = jnp.einsum('bhqk,bkc->bqhc', p[..., start:start + n].astype(c.dtype), c)
        out = part if out is None else out + part
        start += n
    return out


def attend_prompt(q_lat, q_rope, c_kv, k_rope, pos):
    b, s = q_lat.shape[:2]
    nb = s // Q_BLOCK

    def blocks(t):
        return jnp.moveaxis(t.reshape((b, nb, Q_BLOCK) + t.shape[2:]), 1, 0)

    def one_block(args):
        ql, qr, qp = args
        return mla_attend(ql, qr, ((c_kv, k_rope),), qp, pos)

    out = lax.map(one_block, (blocks(q_lat), blocks(q_rope), pos.reshape(nb, Q_BLOCK)))
    return jnp.moveaxis(out, 0, 1).reshape(q_lat.shape)


def attend_sample(past_c, past_kr, q_lat, q_rope, c_kv, k_rope, pos):
    k_pos = jnp.arange(past_c.shape[1] + c_kv.shape[1], dtype=jnp.int32)
    return mla_attend(q_lat, q_rope, ((past_c, past_kr), (c_kv, k_rope)), pos, k_pos)


def _lin_combine(left, right):
    a1, b1 = left
    a2, b2 = right
    return a1 * a2, a2 * b1 + b2


def rg_lru(xr, gate, conv_buf, h0, p):
    xc, new_buf = causal_dwconv(xr, conv_buf, p['rnn_conv_w'], p['rnn_conv_b'])
    b, t, _ = xc.shape
    xb = xc.reshape(b, t, RNN_BLOCKS, RNN_BLOCK)
    r = jax.nn.sigmoid((jnp.einsum('btni,nij->btnj', xb, p['rnn_wa']).reshape(b, t, D_RNN)
                        + p['rnn_ba']).astype(jnp.float32))
    i = jax.nn.sigmoid((jnp.einsum('btni,nij->btnj', xb, p['rnn_wx']).reshape(b, t, D_RNN)
                        + p['rnn_bx']).astype(jnp.float32))
    log_a = -RGLRU_C * r * jax.nn.softplus(-p['rnn_lambda'].astype(jnp.float32))
    a = jnp.exp(log_a)
    u = jnp.sqrt(-jnp.expm1(2.0 * log_a)) * (i * xc.astype(jnp.float32))
    u = u.at[:, 0].add(a[:, 0] * h0.astype(jnp.float32))
    _, h = lax.associative_scan(_lin_combine, (a, u), axis=1)
    y = (h * jax.nn.gelu(gate.astype(jnp.float32))).astype(xr.dtype)
    return y, new_buf, h[:, -1].astype(h0.dtype)


def hybrid_mixer(x, pos, rnn_conv, rnn_h, attend_fn, p):
    b, t, _ = x.shape
    z = x @ p['w_in']
    c_q, c_kv, k_r, xr, gate = jnp.split(z, IN_SPLITS, axis=-1)
    c_q = rms_norm(c_q, p['g_q'])
    c_kv = rms_norm(c_kv, p['g_kv'])
    q = (c_q @ p['w_uq']).reshape(b, t, MLA_HEADS, D_NOPE + D_ROPE)
    q_nope = q[..., :D_NOPE]
    q_rope = rope(q[..., D_NOPE:], pos)
    k_rope = rope(k_r, pos)
    q_lat = jnp.einsum('bthn,chn->bthc', q_nope, p['w_uk'])
    o_lat = attend_fn(q_lat, q_rope, c_kv, k_rope, pos)
    o_attn = jnp.einsum('bthc,chv->bthv', o_lat, p['w_uv']).reshape(b, t, MLA_HEADS * D_V)
    o_rnn, rnn_conv, rnn_h = rg_lru(xr, gate, rnn_conv, rnn_h, p)
    merged = jnp.concatenate([rms_norm(o_attn, p['g_attn_out']), rms_norm(o_rnn, p['g_rnn_out'])], axis=-1)
    return merged @ p['w_out'], c_kv, k_rope, rnn_conv, rnn_h


def memory_kv(mem, p):
    b = mem.shape[0]
    k = (mem @ p['w_k_mem']).reshape(b, N_MEM, MEM_HEADS, MEM_HEAD_DIM)
    v = (mem @ p['w_v_mem']).reshape(b, N_MEM, MEM_HEADS, MEM_HEAD_DIM)
    return k, v


def memory_attend(x, mem_k, mem_v, p):
    b, t, _ = x.shape
    q = (x @ p['w_q_mem']).reshape(b, t, MEM_HEADS, MEM_HEAD_DIM)
    s = jnp.einsum('bthd,bmhd->bhtm', q, mem_k).astype(jnp.float32) * MEM_SCALE
    w = jax.nn.softmax(s, axis=-1).astype(mem_v.dtype)
    o = jnp.einsum('bhtm,bmhd->bthd', w, mem_v).reshape(b, t, D_MODEL)
    return o @ p['w_o_mem']


def conv_ffn(x, buf, p):
    up = x @ p['ffn_w_up']
    a, v = jnp.split(up, 2, axis=-1)
    a, new_buf = causal_dwconv(a, buf, p['ffn_conv_w'], p['ffn_conv_b'])
    return (jax.nn.gelu(a) * v) @ p['ffn_w_down'], new_buf


def decoder_layer(x, pos, rnn_conv, rnn_h, ffn_conv, mem_k, mem_v, attend_fn, p):
    m, c_kv, k_rope, rnn_conv, rnn_h = hybrid_mixer(x, pos, rnn_conv, rnn_h, attend_fn, p)
    x = layer_norm(ALPHA * x + m, p['ln1_g'], p['ln1_b'])
    x = layer_norm(ALPHA * x + memory_attend(x, mem_k, mem_v, p), p['ln2_g'], p['ln2_b'])
    f, ffn_conv = conv_ffn(x, ffn_conv, p)
    x = layer_norm(ALPHA * x + f, p['ln3_g'], p['ln3_b'])
    return x, c_kv, k_rope, rnn_conv, rnn_h, ffn_conv


def setup_inputs(seed: int = 0) -> dict:
    key = jax.random.key(seed)
    ks = iter(jax.random.split(key, 48))
    f32 = jnp.float32
    L = DEPTH

    def nrm(shape, scale):
        return scale * jax.random.normal(next(ks), shape, f32)

    def gain(shape):
        return 1.0 + 0.02 * jax.random.normal(next(ks), shape, f32)

    n_pages = PAST_LEN // PAGE_SIZE
    n_used = DEC_BATCH * n_pages
    n_phys = n_used + n_used // 4
    page_table = jax.random.permutation(next(ks), n_phys)[:n_used].reshape(DEC_BATCH, n_pages).astype(jnp.int32)
    lam_u = jax.random.uniform(next(ks), (L, D_RNN), f32, 0.9, 0.999)
    a0 = lam_u ** (1.0 / RGLRU_C)
    rnn_lambda = jnp.log(a0) - jnp.log1p(-a0)
    return {
        'x_prompt': nrm((BATCH, SEQ, D_MODEL), 1.0),
        'x_sample': nrm((DEC_BATCH, DEC_SEQ, D_MODEL), 1.0),
        'cache_kv_latent': nrm((L, n_phys, PAGE_SIZE, D_C), 1.0),
        'cache_k_rope': nrm((L, n_phys, PAGE_SIZE, D_ROPE), 1.0),
        'cache_mem_k': nrm((L, DEC_BATCH, N_MEM, MEM_HEADS, MEM_HEAD_DIM), 1.0),
        'cache_mem_v': nrm((L, DEC_BATCH, N_MEM, MEM_HEADS, MEM_HEAD_DIM), BETA),
        'state_rnn_conv': nrm((L, DEC_BATCH, RNN_CONV - 1, D_RNN), 1.0),
        'state_rnn_h': nrm((L, DEC_BATCH, D_RNN), 0.5),
        'state_ffn_conv': nrm((L, DEC_BATCH, FFN_CONV - 1, D_FF), BETA),
        'page_table': page_table,
        'mem_prompt': nrm((BATCH, N_MEM, D_MODEL), 1.0),
        'w_in': nrm((L, D_MODEL, D_IN), D_MODEL ** -0.5),
        'g_q': gain((L, Q_LORA)),
        'g_kv': gain((L, D_C)),
        'w_uq': nrm((L, Q_LORA, MLA_HEADS * (D_NOPE + D_ROPE)), Q_LORA ** -0.5),
        'w_uk': nrm((L, D_C, MLA_HEADS, D_NOPE), D_C ** -0.5),
        'w_uv': nrm((L, D_C, MLA_HEADS, D_V), BETA * D_C ** -0.5),
        'rnn_conv_w': nrm((L, RNN_CONV, D_RNN), RNN_CONV ** -0.5),
        'rnn_conv_b': nrm((L, D_RNN), 0.01),
        'rnn_wa': nrm((L, RNN_BLOCKS, RNN_BLOCK, RNN_BLOCK), RNN_BLOCK ** -0.5),
        'rnn_ba': nrm((L, D_RNN), 0.01),
        'rnn_wx': nrm((L, RNN_BLOCKS, RNN_BLOCK, RNN_BLOCK), RNN_BLOCK ** -0.5),
        'rnn_bx': nrm((L, D_RNN), 0.01),
        'rnn_lambda': rnn_lambda,
        'g_attn_out': gain((L, MLA_HEADS * D_V)),
        'g_rnn_out': gain((L, D_RNN)),
        'w_out': nrm((L, D_MIX, D_MODEL), BETA * D_MIX ** -0.5),
        'ln1_g': gain((L, D_MODEL)),
        'ln1_b': nrm((L, D_MODEL), 0.01),
        'w_q_mem': nrm((L, D_MODEL, D_MODEL), D_MODEL ** -0.5),
        'w_k_mem': nrm((L, D_MODEL, D_MODEL), D_MODEL ** -0.5),
        'w_v_mem': nrm((L, D_MODEL, D_MODEL), BETA * D_MODEL ** -0.5),
        'w_o_mem': nrm((L, D_MODEL, D_MODEL), BETA * D_MODEL ** -0.5),
        'ln2_g': gain((L, D_MODEL)),
        'ln2_b': nrm((L, D_MODEL), 0.01),
        'ffn_w_up': nrm((L, D_MODEL, 2 * D_FF), BETA * D_MODEL ** -0.5),
        'ffn_conv_w': nrm((L, FFN_CONV, D_FF), FFN_CONV ** -0.5),
        'ffn_conv_b': nrm((L, D_FF), 0.01),
        'ffn_w_down': nrm((L, D_FF, D_MODEL), BETA * D_FF ** -0.5),
        'ln3_g': gain((L, D_MODEL)),
        'ln3_b': nrm((L, D_MODEL), 0.01),
    }


def reference(x_prompt, x_sample, cache_kv_latent, cache_k_rope, cache_mem_k, cache_mem_v,
              state_rnn_conv, state_rnn_h, state_ffn_conv, page_table, mem_prompt,
              w_in, g_q, g_kv, w_uq, w_uk, w_uv, rnn_conv_w, rnn_conv_b, rnn_wa, rnn_ba,
              rnn_wx, rnn_bx, rnn_lambda, g_attn_out, g_rnn_out, w_out, ln1_g, ln1_b,
              w_q_mem, w_k_mem, w_v_mem, w_o_mem, ln2_g, ln2_b,
              ffn_w_up, ffn_conv_w, ffn_conv_b, ffn_w_down, ln3_g, ln3_b):
    bp, sp, _ = x_prompt.shape
    bs, ts, _ = x_sample.shape
    past_len = page_table.shape[1] * PAGE_SIZE
    pos_p = jnp.arange(sp, dtype=jnp.int32)
    pos_s = past_len + jnp.arange(ts, dtype=jnp.int32)
    hp, hs = x_prompt, x_sample
    p_c, p_kr, p_rc, p_rh, p_mk, p_mv, p_fc = [], [], [], [], [], [], []
    s_c, s_kr, s_rc, s_rh, s_fc = [], [], [], [], []
    for l in range(DEPTH):
        p = {
            'w_in': w_in[l], 'g_q': g_q[l], 'g_kv': g_kv[l], 'w_uq': w_uq[l], 'w_uk': w_uk[l],
            'w_uv': w_uv[l], 'rnn_conv_w': rnn_conv_w[l], 'rnn_conv_b': rnn_conv_b[l],
            'rnn_wa': rnn_wa[l], 'rnn_ba': rnn_ba[l], 'rnn_wx': rnn_wx[l], 'rnn_bx': rnn_bx[l],
            'rnn_lambda': rnn_lambda[l], 'g_attn_out': g_attn_out[l], 'g_rnn_out': g_rnn_out[l],
            'w_out': w_out[l], 'ln1_g': ln1_g[l], 'ln1_b': ln1_b[l],
            'w_q_mem': w_q_mem[l], 'w_k_mem': w_k_mem[l], 'w_v_mem': w_v_mem[l], 'w_o_mem': w_o_mem[l],
            'ln2_g': ln2_g[l], 'ln2_b': ln2_b[l], 'ffn_w_up': ffn_w_up[l], 'ffn_conv_w': ffn_conv_w[l],
            'ffn_conv_b': ffn_conv_b[l], 'ffn_w_down': ffn_w_down[l], 'ln3_g': ln3_g[l], 'ln3_b': ln3_b[l],
        }
        mk, mv = memory_kv(mem_prompt, p)
        hp, c_new, kr_new, rc_new, rh_new, fc_new = decoder_layer(
            hp, pos_p,
            jnp.zeros((bp, RNN_CONV - 1, D_RNN), hp.dtype), jnp.zeros((bp, D_RNN), hp.dtype),
            jnp.zeros((bp, FFN_CONV - 1, D_FF), hp.dtype), mk, mv, attend_prompt, p)
        p_c.append(c_new); p_kr.append(kr_new); p_rc.append(rc_new); p_rh.append(rh_new)
        p_mk.append(mk); p_mv.append(mv); p_fc.append(fc_new)
        past_c = cache_kv_latent[l][page_table].reshape(bs, past_len, D_C)
        past_kr = cache_k_rope[l][page_table].reshape(bs, past_len, D_ROPE)
        hs, c_new, kr_new, rc_new, rh_new, fc_new = decoder_layer(
            hs, pos_s, state_rnn_conv[l], state_rnn_h[l], state_ffn_conv[l],
            cache_mem_k[l], cache_mem_v[l], functools.partial(attend_sample, past_c, past_kr), p)
        s_c.append(c_new); s_kr.append(kr_new); s_rc.append(rc_new); s_rh.append(rh_new); s_fc.append(fc_new)
    return (hp, hs,
            jnp.stack(p_c), jnp.stack(p_kr), jnp.stack(p_rc), jnp.stack(p_rh),
            jnp.stack(p_mk), jnp.stack(p_mv), jnp.stack(p_fc),
            jnp.stack(s_c), jnp.stack(s_kr), jnp.stack(s_rc), jnp.stack(s_rh), jnp.stack(s_fc))
```

```python
import functools

import jax
import jax.numpy as jnp
from jax import lax
from jax.experimental import pallas as pl
from jax.experimental.pallas import tpu as pltpu

D_MODEL = 1024
MLA_HEADS = 8
D_NOPE = 64
D_ROPE = 32
D_V = 64
D_C = 256
Q_LORA = 512
D_RNN = 512
RNN_BLOCKS = 8
RNN_CONV = 4
RGLRU_C = 8.0
N_MEM = 256
MEM_HEADS = 4
MEM_HEAD_DIM = 256
D_FF = 2816
FFN_CONV = 3
PAGE_SIZE = 128
ROPE_BASE = 10000.0
MLA_SCALE = (D_NOPE + D_ROPE) ** -0.5
MEM_SCALE = MEM_HEAD_DIM ** -0.5
ALPHA = 2.0 ** 0.25
LN_EPS = 1e-5
RMS_EPS = 1e-6
NEG_INF = -1e30

LANES = 128
SUBLANES = 8
VMEM_LIMIT_BYTES = 56 * 1024 * 1024

TOKEN_TILE = 256
FLASH_TILE = 512
FF_CHUNK = 256
PAGES_PER_CHUNK = 16
HEAD_PAD = 128
BF16 = jnp.bfloat16
F32 = jnp.float32


def _dot(a, b):
    return jnp.dot(a, b, preferred_element_type=F32)


def _dot_nt(a, b):
    return lax.dot_general(a, b, (((1,), (1,)), ((), ())), preferred_element_type=F32)


def _rms(x, g):
    return x * lax.rsqrt(jnp.mean(x * x, axis=-1, keepdims=True) + RMS_EPS) * g


def _layer_norm(x, g, b):
    mu = jnp.mean(x, axis=-1, keepdims=True)
    xc = x - mu
    var = jnp.mean(xc * xc, axis=-1, keepdims=True)
    return xc * lax.rsqrt(var + LN_EPS) * g + b


def _gelu(x):
    return 0.5 * x * (1.0 + jnp.tanh(0.7978845608028654 * (x + 0.044715 * (x * x * x))))


def _row_in_seq(rows, cols, seq_len):
    r = lax.broadcasted_iota(jnp.int32, (rows, cols), 0)
    assert seq_len & (seq_len - 1) == 0, "sequence length must be a power of two"
    return r & (seq_len - 1) if seq_len < rows else r


def _shifted_rows(x, hist, d, width, seq_len, win_ref):
    rows, cols = x.shape
    if seq_len >= rows:
        return win_ref[pl.ds(SUBLANES - d, rows), :]
    j = _row_in_seq(rows, cols, seq_len)
    back = (d - (width - 1)) % rows
    from_hist = hist if back == 0 else pltpu.roll(hist, back, 0)
    return jnp.where(j >= d, pltpu.roll(x, d, 0), from_hist)


def _inproj_kernel(prompt, x_ref, win_ref, gq_ref, gkv_ref, ck_ref, sk_ref, cq_ref, sq_ref, *refs):
    xb = x_ref[...].astype(BF16)
    z = _dot(xb, win_ref[...])
    c_q = _rms(z[:, 0:512], gq_ref[...])
    c_kv = _rms(z[:, 512:768], gkv_ref[...])
    xr = z[:, 768:1280]
    gate = z[:, 1280:1792]
    k_rope = z[:, 1792:1824] * ck_ref[...] + z[:, 1920:1952] * sk_ref[...]
    cqb = c_q.astype(BF16)
    ckvb = c_kv.astype(BF16)
    if prompt:
        (wq_ref, wqs_ref, wkn_ref, wv_ref, ck128_ref, sk128_ref,
         ckv_out, kr_out, xr_out, gate_out, q_out, k_out, v_out) = refs
        q = _dot(cqb, wq_ref[...])
        qs = _dot(cqb, wqs_ref[...])
        kn = _dot(ckvb, wkn_ref[...])
        kr_placed = z[:, 2048:2176] * ck128_ref[...] + z[:, 2176:2304] * sk128_ref[...]
        cq_t = cq_ref[...]
        sq_t = sq_ref[...]
        for h in range(MLA_HEADS):
            sl = slice(h * HEAD_PAD, (h + 1) * HEAD_PAD)
            q_out[:, sl] = (q[:, sl] * cq_t + qs[:, sl] * sq_t).astype(BF16)
            k_out[:, sl] = (kn[:, sl] + kr_placed).astype(BF16)
        v_out[...] = _dot(ckvb, wv_ref[...]).astype(BF16)
    else:
        (wqn_ref, wqr_ref, wqrs_ref, wukt_ref,
         ckv_out, kr_out, xr_out, gate_out, qlat_out, qrope_out) = refs
        qn = _dot(cqb, wqn_ref[...]).astype(BF16)
        qr = _dot(cqb, wqr_ref[...])
        qrs = _dot(cqb, wqrs_ref[...])
        qrope_out[...] = (qr * cq_ref[...] + qrs * sq_ref[...]).astype(BF16)
        for h in range(MLA_HEADS):
            qlat_out[:, h * D_C:(h + 1) * D_C] = _dot(
                qn[:, h * D_NOPE:(h + 1) * D_NOPE], wukt_ref[h]).astype(BF16)
    ckv_out[...] = c_kv
    kr_out[...] = k_rope
    xr_out[...] = xr
    gate_out[...] = gate


def _rglru_kernel(seq_len, tiles_per_seq, xr_ref, gate_ref, hist_ref, h0_ref, cw_ref, cb_ref,
                  wa_ref, ba_ref, wx_ref, bx_ref, lam_ref, y_out, h_out, win_ref, hc_ref):
    rows = xr_ref.shape[0]
    long_seq = seq_len >= rows
    x = xr_ref[...]
    if long_seq:
        first = pl.program_id(0) % tiles_per_seq == 0

        @pl.when(first)
        def _():
            win_ref[0:SUBLANES, :] = jnp.zeros((SUBLANES, D_RNN), F32)
            hc_ref[...] = jnp.zeros((1, D_RNN), F32)

        win_ref[SUBLANES:, :] = x
        hist = None
    else:
        hist = hist_ref[...]
    cw = cw_ref[...]
    xc = cb_ref[...] + x * cw[RNN_CONV - 1:RNN_CONV, :]
    for d in range(1, RNN_CONV):
        xc = xc + _shifted_rows(x, hist, d, RNN_CONV, seq_len, win_ref) * cw[RNN_CONV - 1 - d:RNN_CONV - d, :]
    if long_seq:
        win_ref[0:SUBLANES, :] = x[rows - SUBLANES:, :]

    xcb = xc.astype(BF16)
    half = D_RNN // 2
    ra = jnp.concatenate([_dot(xcb[:, :half], wa_ref[0]), _dot(xcb[:, half:], wa_ref[1])], axis=1)
    rx = jnp.concatenate([_dot(xcb[:, :half], wx_ref[0]), _dot(xcb[:, half:], wx_ref[1])], axis=1)
    r = jax.nn.sigmoid(ra + ba_ref[...])
    gate_i = jax.nn.sigmoid(rx + bx_ref[...])
    z = -lam_ref[...]
    softplus = jnp.maximum(z, 0.0) + jnp.log1p(jnp.exp(-jnp.abs(z)))
    a = jnp.exp(-RGLRU_C * r * softplus)
    u = jnp.sqrt(1.0 - a * a) * (gate_i * xc)

    j = _row_in_seq(rows, D_RNN, seq_len)
    if not long_seq:
        u = u + jnp.where(j == 0, a * h0_ref[...], 0.0)
    s = 1
    while s < min(seq_len, rows):
        keep = j >= s
        u = jnp.where(keep, a * pltpu.roll(u, s, 0) + u, u)
        a = jnp.where(keep, a * pltpu.roll(a, s, 0), a)
        s *= 2
    if long_seq:
        h = a * hc_ref[...] + u
        hc_ref[...] = h[rows - 1:rows, :]
        h_out[0] = h[rows - 1:rows, :]
    else:
        h = u
        h_out[...] = h
    y_out[...] = h * _gelu(gate_ref[...])


def _flash_kernel(q_ref, k_ref, v_ref, o_ref, m_scr, l_scr, acc_scr):
    qi = pl.program_id(2)
    ki = pl.program_id(3)
    tq = q_ref.shape[0]
    tk = k_ref.shape[0]

    @pl.when(ki == 0)
    def _():
        m_scr[...] = jnp.full(m_scr.shape, NEG_INF, F32)
        l_scr[...] = jnp.zeros(l_scr.shape, F32)
        acc_scr[...] = jnp.zeros(acc_scr.shape, F32)

    @pl.when(ki <= qi)
    def _():
        row = qi * tq + lax.broadcasted_iota(jnp.int32, (tq, tk), 0)
        col = ki * tk + lax.broadcasted_iota(jnp.int32, (tq, tk), 1)
        visible = col <= row
        for hh in range(2):
            q = q_ref[:, hh * HEAD_PAD:(hh + 1) * HEAD_PAD]
            k = k_ref[:, hh * HEAD_PAD:(hh + 1) * HEAD_PAD]
            s = jnp.where(visible, _dot_nt(q, k) * MLA_SCALE, NEG_INF)
            m_prev = m_scr[hh][:, 0:1]
            m_new = jnp.maximum(m_prev, jnp.max(s, axis=-1, keepdims=True))
            alpha = jnp.exp(m_prev - m_new)
            p = jnp.exp(s - m_new)
            l_scr[hh] = jnp.broadcast_to(alpha * l_scr[hh][:, 0:1] + jnp.sum(p, axis=-1, keepdims=True),
                                         (tq, LANES))
            m_scr[hh] = jnp.broadcast_to(m_new, (tq, LANES))
            acc_scr[hh] = alpha * acc_scr[hh] + _dot(p.astype(BF16), v_ref[:, hh * D_V:(hh + 1) * D_V])

    @pl.when(ki == pl.num_programs(3) - 1)
    def _():
        for hh in range(2):
            o_ref[:, hh * D_V:(hh + 1) * D_V] = acc_scr[hh] / l_scr[hh][:, 0:1]


def _decode_kernel(pt_ref, ql_ref, qr_ref, cn_ref, kn_ref, ckv_hbm, ckr_hbm, o_ref,
                   cbuf, rbuf, sem, m_scr, l_scr, acc_scr, *, n_chunks):
    b = pl.program_id(0)
    j = pl.program_id(1)
    n = b * n_chunks + j
    total = pl.num_programs(0) * n_chunks
    slot = n % 2
    cpc = cbuf.shape[1]
    rows = ql_ref.shape[1]

    def chunk_copies(bb, jj, sl):
        copies = []
        for p in range(cpc):
            page = pt_ref[bb, jj * cpc + p]
            copies.append(pltpu.make_async_copy(ckv_hbm.at[page], cbuf.at[sl, p], sem.at[sl, 0]))
            copies.append(pltpu.make_async_copy(ckr_hbm.at[page], rbuf.at[sl, p], sem.at[sl, 1]))
        return copies

    @pl.when(n == 0)
    def _():
        for c in chunk_copies(0, 0, 0):
            c.start()

    @pl.when(n + 1 < total)
    def _():
        nxt = n + 1
        for c in chunk_copies(nxt // n_chunks, nxt % n_chunks, 1 - slot):
            c.start()

    @pl.when(j == 0)
    def _():
        m_scr[...] = jnp.full(m_scr.shape, NEG_INF, F32)
        l_scr[...] = jnp.zeros(l_scr.shape, F32)
        acc_scr[...] = jnp.zeros(acc_scr.shape, F32)

    for c in chunk_copies(b, j, slot):
        c.wait()

    ql = ql_ref[0]
    qr = qr_ref[0]

    def attend(cb, krb, visible):
        s = (_dot_nt(ql, cb) + _dot_nt(qr, krb)) * MLA_SCALE
        if visible is not None:
            s = jnp.where(visible, s, NEG_INF)
        m_prev = m_scr[:, 0:1]
        m_new = jnp.maximum(m_prev, jnp.max(s, axis=-1, keepdims=True))
        alpha = jnp.exp(m_prev - m_new)
        p = jnp.exp(s - m_new)
        l_scr[...] = jnp.broadcast_to(alpha * l_scr[:, 0:1] + jnp.sum(p, axis=-1, keepdims=True),
                                      l_scr.shape)
        m_scr[...] = jnp.broadcast_to(m_new, m_scr.shape)
        acc_scr[...] = alpha * acc_scr[...] + _dot(p.astype(BF16), cb)

    keys = cpc * PAGE_SIZE
    attend(cbuf[slot].reshape(keys, D_C).astype(BF16),
           rbuf[slot].reshape(keys, D_ROPE).astype(BF16), None)

    @pl.when(j == n_chunks - 1)
    def _():
        npad = cn_ref.shape[1]
        tok = lax.broadcasted_iota(jnp.int32, (rows, npad), 0) // MLA_HEADS
        key = lax.broadcasted_iota(jnp.int32, (rows, npad), 1)
        attend(cn_ref[0], kn_ref[0], key <= tok)
        o_ref[0] = acc_scr[...] / l_scr[:, 0:1]


def _outproj_kernel(prompt, x_ref, oa_ref, yr_ref, ga_ref, gr_ref, woa_ref, wor_ref, g1_ref, b1_ref,
                    wqm_ref, *refs):
    if prompt:
        x1_out, qm_out = refs
        o_attn = oa_ref[...]
    else:
        wuv_ref, x1_out, qm_out = refs
        o_lat = oa_ref[...].astype(BF16)
        o_attn = jnp.concatenate(
            [_dot(o_lat[:, h * D_C:(h + 1) * D_C], wuv_ref[h]) for h in range(MLA_HEADS)], axis=1)
    na = _rms(o_attn, ga_ref[...]).astype(BF16)
    nr = _rms(yr_ref[...], gr_ref[...]).astype(BF16)
    m = _dot(na, woa_ref[...]) + _dot(nr, wor_ref[...])
    x1 = _layer_norm(ALPHA * x_ref[...] + m, g1_ref[...], b1_ref[...])
    x1_out[...] = x1
    qm_out[...] = _dot(x1.astype(BF16), wqm_ref[...]).astype(BF16)


def _memattn_kernel(q_ref, k_ref, v_ref, o_ref):
    for bb in range(q_ref.shape[0]):
        for h in range(MEM_HEADS):
            sl = slice(h * MEM_HEAD_DIM, (h + 1) * MEM_HEAD_DIM)
            q = q_ref[bb, :, sl]
            k = k_ref[bb, :, sl].astype(BF16)
            v = v_ref[bb, :, sl].astype(BF16)
            s = _dot_nt(q, k) * MEM_SCALE
            p = jnp.exp(s - jnp.max(s, axis=-1, keepdims=True))
            w = (p / jnp.sum(p, axis=-1, keepdims=True)).astype(BF16)
            o_ref[bb, :, sl] = _dot(w, v).astype(BF16)


def _matmul_kernel(x_ref, w_ref, o_ref):
    o_ref[...] = _dot(x_ref[...].astype(BF16), w_ref[...])


def _ffn_kernel(seq_len, tiles_per_seq, x1_ref, om_ref, hist_ref, wom_ref, g2_ref, b2_ref,
                wa_ref, wv_ref, cw_ref, cb_ref, wd_ref, g3_ref, b3_ref,
                y_out, a_out, win_ref, carry_ref, acc_ref):
    rows = x1_ref.shape[0]
    long_seq = seq_len >= rows
    x2 = _layer_norm(ALPHA * x1_ref[...] + _dot(om_ref[...], wom_ref[...]), g2_ref[...], b2_ref[...])
    x2b = x2.astype(BF16)
    if long_seq:
        @pl.when(pl.program_id(0) % tiles_per_seq == 0)
        def _():
            carry_ref[...] = jnp.zeros(carry_ref.shape, F32)

    for c in range(D_FF // FF_CHUNK):
        sl = slice(c * FF_CHUNK, (c + 1) * FF_CHUNK)
        a = _dot(x2b, wa_ref[:, sl])
        v = _dot(x2b, wv_ref[:, sl])
        if long_seq:
            win_ref[0:SUBLANES, :] = carry_ref[:, sl]
            win_ref[SUBLANES:, :] = a
            carry_ref[:, sl] = a[rows - SUBLANES:, :]
            a_out[0, :, sl] = a[rows - (FFN_CONV - 1):, :]
            hist = None
        else:
            a_out[:, sl] = a
            hist = hist_ref[:, sl]
        cw = cw_ref[:, sl]
        ac = cb_ref[:, sl] + a * cw[FFN_CONV - 1:FFN_CONV, :]
        for d in range(1, FFN_CONV):
            ac = ac + _shifted_rows(a, hist, d, FFN_CONV, seq_len, win_ref) * cw[FFN_CONV - 1 - d:FFN_CONV - d, :]
        hidden = (_gelu(ac) * v).astype(BF16)
        part = _dot(hidden, wd_ref[sl, :])
        if c == 0:
            acc_ref[...] = part
        else:
            acc_ref[...] += part
    y_out[...] = _layer_norm(ALPHA * x2 + acc_ref[...], g3_ref[...], b3_ref[...])


def _rows(cols, tile=TOKEN_TILE):
    return pl.BlockSpec((tile, cols), lambda i: (i, 0))


def _whole(shape):
    nd = len(shape)
    return pl.BlockSpec(shape, lambda *_: (0,) * nd, pipeline_mode=pl.Buffered(1))


def _params(n_axes):
    return pltpu.CompilerParams(dimension_semantics=("arbitrary",) * n_axes,
                                vmem_limit_bytes=VMEM_LIMIT_BYTES)


def _rope_tables(pos):
    half = D_ROPE // 2
    inv_freq = ROPE_BASE ** (-jnp.arange(half, dtype=F32) / half)
    ang = pos.astype(F32)[:, None] * inv_freq[None, :]
    cos = jnp.cos(ang)
    sin = jnp.sin(ang)
    return jnp.concatenate([cos, cos], axis=1), jnp.concatenate([-sin, sin], axis=1)


def _swap_halves(w):
    half = D_ROPE // 2
    return jnp.concatenate([w[..., half:], w[..., :half]], axis=-1)


def _block_diag_pairs(w):
    per = RNN_BLOCKS // 2
    blk = D_RNN // RNN_BLOCKS
    w4 = w.reshape(2, per, blk, blk)
    eye = jnp.eye(per, dtype=w.dtype)
    return jnp.einsum('knij,nm->knimj', w4, eye).reshape(2, per * blk, per * blk).astype(BF16)


def _row2(v):
    return v.reshape(1, -1)


def _group_forward(prompt, x3, pos, p, mem_k, mem_v, rnn_hist, rnn_h0, ffn_hist, decode_inputs):
    nb, seq_len, _ = x3.shape
    n = nb * seq_len
    tile = TOKEN_TILE
    n_tiles = n // tile
    long_seq = seq_len >= tile
    tiles_per_seq = seq_len // tile if long_seq else 1
    x = x3.reshape(n, D_MODEL)

    c32, s32 = _rope_tables(pos)
    reps = 1 if long_seq else tile // seq_len
    c32 = jnp.tile(c32, (reps, 1))
    s32 = jnp.tile(s32, (reps, 1))
    tab_rows = c32.shape[0]
    tab_blocks = tab_rows // tile

    def tab_spec(cols):
        return pl.BlockSpec((tile, cols), lambda i: (i % tab_blocks, 0))

    w_in = p['w_in']
    w_cq, w_ckv, w_kr = w_in[:, 0:512], w_in[:, 512:768], w_in[:, 768:800]
    w_xr, w_gate = w_in[:, 800:1312], w_in[:, 1312:1824]
    w_krs = _swap_halves(w_kr)
    zeros = lambda c: jnp.zeros((D_MODEL, c), F32)
    w_in_cols = [w_cq, w_ckv, w_xr, w_gate, w_kr, zeros(96), w_krs, zeros(96)]
    w_uq = p['w_uq'].reshape(Q_LORA, MLA_HEADS, D_NOPE + D_ROPE)
    w_uq_nope, w_uq_rope = w_uq[..., :D_NOPE], w_uq[..., D_NOPE:]

    if prompt:
        w_in_cols += [zeros(64), w_kr, zeros(32), zeros(64), w_krs, zeros(32)]
        ones_z = lambda t, fill: jnp.concatenate(
            [jnp.full((tab_rows, D_NOPE), fill, F32), t, jnp.zeros((tab_rows, 32), F32)], axis=1)
        cq_tab, sq_tab = ones_z(c32, 1.0), ones_z(s32, 0.0)
        ck128, sk128 = ones_z(c32, 0.0), ones_z(s32, 0.0)
        pad_q = jnp.zeros((Q_LORA, MLA_HEADS, HEAD_PAD - D_NOPE - D_ROPE), F32)
        wq = jnp.concatenate([w_uq_nope, w_uq_rope, pad_q], axis=-1).reshape(Q_LORA, -1).astype(BF16)
        wqs = jnp.concatenate([jnp.zeros_like(w_uq_nope), _swap_halves(w_uq_rope), pad_q],
                              axis=-1).reshape(Q_LORA, -1).astype(BF16)
        wkn = jnp.concatenate([p['w_uk'], jnp.zeros((D_C, MLA_HEADS, HEAD_PAD - D_NOPE), F32)],
                              axis=-1).reshape(D_C, -1).astype(BF16)
        wv = p['w_uv'].reshape(D_C, MLA_HEADS * D_V).astype(BF16)
        extra_in = [wq, wqs, wkn, wv, ck128, sk128]
        extra_specs = [_whole(wq.shape), _whole(wqs.shape), _whole(wkn.shape), _whole(wv.shape),
                       tab_spec(HEAD_PAD), tab_spec(HEAD_PAD)]
        q_cols = HEAD_PAD
        extra_out = [jax.ShapeDtypeStruct((n, MLA_HEADS * HEAD_PAD), BF16),
                     jax.ShapeDtypeStruct((n, MLA_HEADS * HEAD_PAD), BF16),
                     jax.ShapeDtypeStruct((n, MLA_HEADS * D_V), BF16)]
        extra_out_specs = [_rows(MLA_HEADS * HEAD_PAD), _rows(MLA_HEADS * HEAD_PAD), _rows(MLA_HEADS * D_V)]
    else:
        cq_tab = jnp.tile(c32, (1, MLA_HEADS))
        sq_tab = jnp.tile(s32, (1, MLA_HEADS))
        wqn = w_uq_nope.reshape(Q_LORA, -1).astype(BF16)
        wqr = w_uq_rope.reshape(Q_LORA, -1).astype(BF16)
        wqrs = _swap_halves(w_uq_rope).reshape(Q_LORA, -1).astype(BF16)
        wukt = jnp.transpose(p['w_uk'], (1, 2, 0)).astype(BF16)
        extra_in = [wqn, wqr, wqrs, wukt]
        extra_specs = [_whole(w.shape) for w in extra_in]
        q_cols = MLA_HEADS * D_ROPE
        extra_out = [jax.ShapeDtypeStruct((n, MLA_HEADS * D_C), BF16),
                     jax.ShapeDtypeStruct((n, MLA_HEADS * D_ROPE), BF16)]
        extra_out_specs = [_rows(MLA_HEADS * D_C), _rows(MLA_HEADS * D_ROPE)]
    w_in_all = jnp.concatenate(w_in_cols, axis=1).astype(BF16)

    inproj_out = pl.pallas_call(
        functools.partial(_inproj_kernel, prompt),
        grid=(n_tiles,),
        in_specs=[_rows(D_MODEL), _whole(w_in_all.shape), _whole((1, Q_LORA)), _whole((1, D_C)),
                  tab_spec(D_ROPE), tab_spec(D_ROPE), tab_spec(q_cols), tab_spec(q_cols)] + extra_specs,
        out_specs=[_rows(D_C), _rows(D_ROPE), _rows(D_RNN), _rows(D_RNN)] + extra_out_specs,
        out_shape=[jax.ShapeDtypeStruct((n, D_C), F32), jax.ShapeDtypeStruct((n, D_ROPE), F32),
                   jax.ShapeDtypeStruct((n, D_RNN), F32), jax.ShapeDtypeStruct((n, D_RNN), F32)] + extra_out,
        compiler_params=_params(1),
        name='inproj_prompt' if prompt else 'inproj_sample',
    )(x, w_in_all, _row2(p['g_q']), _row2(p['g_kv']), c32, s32, cq_tab, sq_tab, *extra_in)
    c_kv, k_rope, xr, gate = inproj_out[:4]

    if long_seq:
        h_shape, h_spec = (nb, 1, D_RNN), pl.BlockSpec((1, 1, D_RNN), lambda i: (i // tiles_per_seq, 0, 0))
        rnn_hist = jnp.zeros((SUBLANES, D_RNN), F32)
        rnn_h0 = jnp.zeros((SUBLANES, D_RNN), F32)
        hist_spec = _whole((SUBLANES, D_RNN))
    else:
        h_shape, h_spec = (n, D_RNN), _rows(D_RNN)
        hist_spec = _rows(D_RNN)
    y_rnn, h_all = pl.pallas_call(
        functools.partial(_rglru_kernel, seq_len, tiles_per_seq),
        grid=(n_tiles,),
        in_specs=[_rows(D_RNN), _rows(D_RNN), hist_spec, hist_spec,
                  _whole((RNN_CONV, D_RNN)), _whole((1, D_RNN)),
                  _whole((2, D_RNN // 2, D_RNN // 2)), _whole((1, D_RNN)),
                  _whole((2, D_RNN // 2, D_RNN // 2)), _whole((1, D_RNN)), _whole((1, D_RNN))],
        out_specs=[_rows(D_RNN), h_spec],
        out_shape=[jax.ShapeDtypeStruct((n, D_RNN), F32), jax.ShapeDtypeStruct(h_shape, F32)],
        scratch_shapes=[pltpu.VMEM((tile + SUBLANES, D_RNN), F32), pltpu.VMEM((1, D_RNN), F32)],
        compiler_params=_params(1),
        name='rglru_prompt' if prompt else 'rglru_sample',
    )(xr, gate, rnn_hist, rnn_h0, p['rnn_conv_w'], _row2(p['rnn_conv_b']),
      _block_diag_pairs(p['rnn_wa']), _row2(p['rnn_ba']),
      _block_diag_pairs(p['rnn_wx']), _row2(p['rnn_bx']), _row2(p['rnn_lambda']))
    xr3 = xr.reshape(nb, seq_len, D_RNN)
    rnn_conv_new = xr3[:, seq_len - (RNN_CONV - 1):, :]
    if long_seq:
        rnn_h_new = h_all.reshape(nb, D_RNN)
    else:
        rnn_h_new = h_all.reshape(nb, seq_len, D_RNN)[:, seq_len - 1, :]

    if prompt:
        q, k, v = inproj_out[4:]
        ft = FLASH_TILE
        nq = seq_len // ft
        o_attn = pl.pallas_call(
            _flash_kernel,
            grid=(nb, MLA_HEADS // 2, nq, nq),
            in_specs=[pl.BlockSpec((ft, 2 * HEAD_PAD), lambda b, h, qi, ki: (b * nq + qi, h)),
                      pl.BlockSpec((ft, 2 * HEAD_PAD), lambda b, h, qi, ki: (b * nq + jnp.minimum(ki, qi), h)),
                      pl.BlockSpec((ft, 2 * D_V), lambda b, h, qi, ki: (b * nq + jnp.minimum(ki, qi), h))],
            out_specs=pl.BlockSpec((ft, 2 * D_V), lambda b, h, qi, ki: (b * nq + qi, h)),
            out_shape=jax.ShapeDtypeStruct((n, MLA_HEADS * D_V), F32),
            scratch_shapes=[pltpu.VMEM((2, ft, LANES), F32), pltpu.VMEM((2, ft, LANES), F32),
                            pltpu.VMEM((2, ft, D_V), F32)],
            compiler_params=_params(4),
            name='mla_prompt',
        )(q, k, v)
        attn_in = o_attn
        attn_cols = MLA_HEADS * D_V
    else:
        qlat, qrope = inproj_out[4:]
        page_table, cache_c, cache_kr = decode_inputs
        rows = seq_len * MLA_HEADS
        n_chunks = page_table.shape[1] // PAGES_PER_CHUNK
        npad = LANES
        pad_new = lambda t: jnp.pad(t.reshape(nb, seq_len, -1),
                                    ((0, 0), (0, npad - seq_len), (0, 0))).astype(BF16)
        o_lat = pl.pallas_call(
            functools.partial(_decode_kernel, n_chunks=n_chunks),
            grid_spec=pltpu.PrefetchScalarGridSpec(
                num_scalar_prefetch=1,
                grid=(nb, n_chunks),
                in_specs=[pl.BlockSpec((1, rows, D_C), lambda b, j, pt: (b, 0, 0)),
                          pl.BlockSpec((1, rows, D_ROPE), lambda b, j, pt: (b, 0, 0)),
                          pl.BlockSpec((1, npad, D_C), lambda b, j, pt: (b, 0, 0)),
                          pl.BlockSpec((1, npad, D_ROPE), lambda b, j, pt: (b, 0, 0)),
                          pl.BlockSpec(memory_space=pl.ANY),
                          pl.BlockSpec(memory_space=pl.ANY)],
                out_specs=pl.BlockSpec((1, rows, D_C), lambda b, j, pt: (b, 0, 0)),
                scratch_shapes=[pltpu.VMEM((2, PAGES_PER_CHUNK, PAGE_SIZE, D_C), F32),
                                pltpu.VMEM((2, PAGES_PER_CHUNK, PAGE_SIZE, D_ROPE), F32),
                                pltpu.SemaphoreType.DMA((2, 2)),
                                pltpu.VMEM((rows, LANES), F32), pltpu.VMEM((rows, LANES), F32),
                                pltpu.VMEM((rows, D_C), F32)]),
            out_shape=jax.ShapeDtypeStruct((nb, rows, D_C), F32),
            compiler_params=_params(2),
            name='mla_sample',
        )(page_table, qlat.reshape(nb, rows, D_C), qrope.reshape(nb, rows, D_ROPE),
          pad_new(c_kv), pad_new(k_rope), cache_c, cache_kr)
        attn_in = o_lat.reshape(n, MLA_HEADS * D_C)
        attn_cols = MLA_HEADS * D_C

    w_out = p['w_out'].astype(BF16)
    half_mix = MLA_HEADS * D_V
    out_extra_in, out_extra_specs = [], []
    if not prompt:
        wuv = jnp.transpose(p['w_uv'], (1, 0, 2)).astype(BF16)
        out_extra_in, out_extra_specs = [wuv], [_whole(wuv.shape)]
    x1, qm = pl.pallas_call(
        functools.partial(_outproj_kernel, prompt),
        grid=(n_tiles,),
        in_specs=[_rows(D_MODEL), _rows(attn_cols), _rows(D_RNN), _whole((1, half_mix)), _whole((1, D_RNN)),
                  _whole((half_mix, D_MODEL)), _whole((D_RNN, D_MODEL)),
                  _whole((1, D_MODEL)), _whole((1, D_MODEL)), _whole((D_MODEL, D_MODEL))] + out_extra_specs,
        out_specs=[_rows(D_MODEL), _rows(D_MODEL)],
        out_shape=[jax.ShapeDtypeStruct((n, D_MODEL), F32), jax.ShapeDtypeStruct((n, D_MODEL), BF16)],
        compiler_params=_params(1),
        name='outproj_prompt' if prompt else 'outproj_sample',
    )(x, attn_in, y_rnn, _row2(p['g_attn_out']), _row2(p['g_rnn_out']),
      w_out[:half_mix], w_out[half_mix:], _row2(p['ln1_g']), _row2(p['ln1_b']),
      p['w_q_mem'].astype(BF16), *out_extra_in)

    if long_seq:
        seqs_per_step, q_rows = 1, FLASH_TILE
    else:
        seqs_per_step, q_rows = 4, seq_len
    q_tiles = seq_len // q_rows
    o_mem = pl.pallas_call(
        _memattn_kernel,
        grid=(nb // seqs_per_step, q_tiles),
        in_specs=[pl.BlockSpec((seqs_per_step, q_rows, D_MODEL), lambda b, t: (b, t, 0)),
                  pl.BlockSpec((seqs_per_step, N_MEM, D_MODEL), lambda b, t: (b, 0, 0)),
                  pl.BlockSpec((seqs_per_step, N_MEM, D_MODEL), lambda b, t: (b, 0, 0))],
        out_specs=pl.BlockSpec((seqs_per_step, q_rows, D_MODEL), lambda b, t: (b, t, 0)),
        out_shape=jax.ShapeDtypeStruct((nb, seq_len, D_MODEL), BF16),
        compiler_params=_params(2),
        name='memattn_prompt' if prompt else 'memattn_sample',
    )(qm.reshape(nb, seq_len, D_MODEL), mem_k.reshape(nb, N_MEM, D_MODEL), mem_v.reshape(nb, N_MEM, D_MODEL))

    w_up = p['ffn_w_up'].astype(BF16)
    if long_seq:
        a_shape = (nb, FFN_CONV - 1, D_FF)
        a_spec = pl.BlockSpec((1, FFN_CONV - 1, D_FF), lambda i: (i // tiles_per_seq, 0, 0))
        ffn_hist = jnp.zeros((SUBLANES, D_FF), F32)
        fh_spec = _whole((SUBLANES, D_FF))
    else:
        a_shape, a_spec = (n, D_FF), _rows(D_FF)
        fh_spec = _rows(D_FF)
    y, a_state = pl.pallas_call(
        functools.partial(_ffn_kernel, seq_len, tiles_per_seq),
        grid=(n_tiles,),
        in_specs=[_rows(D_MODEL), _rows(D_MODEL), fh_spec, _whole((D_MODEL, D_MODEL)),
                  _whole((1, D_MODEL)), _whole((1, D_MODEL)),
                  _whole((D_MODEL, D_FF)), _whole((D_MODEL, D_FF)),
                  _whole((FFN_CONV, D_FF)), _whole((1, D_FF)), _whole((D_FF, D_MODEL)),
                  _whole((1, D_MODEL)), _whole((1, D_MODEL))],
        out_specs=[_rows(D_MODEL), a_spec],
        out_shape=[jax.ShapeDtypeStruct((n, D_MODEL), F32), jax.ShapeDtypeStruct(a_shape, F32)],
        scratch_shapes=[pltpu.VMEM((tile + SUBLANES, FF_CHUNK), F32), pltpu.VMEM((SUBLANES, D_FF), F32),
                        pltpu.VMEM((tile, D_MODEL), F32)],
        compiler_params=_params(1),
        name='ffn_prompt' if prompt else 'ffn_sample',
    )(x1, o_mem.reshape(n, D_MODEL), ffn_hist, p['w_o_mem'].astype(BF16),
      _row2(p['ln2_g']), _row2(p['ln2_b']), w_up[:, :D_FF], w_up[:, D_FF:],
      p['ffn_conv_w'], _row2(p['ffn_conv_b']), p['ffn_w_down'].astype(BF16),
      _row2(p['ln3_g']), _row2(p['ln3_b']))
    if long_seq:
        ffn_conv_new = a_state
    else:
        ffn_conv_new = a_state.reshape(nb, seq_len, D_FF)[:, seq_len - (FFN_CONV - 1):, :]

    return (y.reshape(nb, seq_len, D_MODEL), c_kv.reshape(nb, seq_len, D_C),
            k_rope.reshape(nb, seq_len, D_ROPE), rnn_conv_new, rnn_h_new, ffn_conv_new)


def _seq_history(state, seq_len):
    nb, w, c = state.shape
    return jnp.pad(state, ((0, 0), (0, seq_len - w), (0, 0))).reshape(nb * seq_len, c)


def kernel(x_prompt, x_sample, cache_kv_latent, cache_k_rope, cache_mem_k, cache_mem_v, state_rnn_conv, state_rnn_h, state_ffn_conv, page_table, mem_prompt, w_in, g_q, g_kv, w_uq, w_uk, w_uv, rnn_conv_w, rnn_conv_b, rnn_wa, rnn_ba, rnn_wx, rnn_bx, rnn_lambda, g_attn_out, g_rnn_out, w_out, ln1_g, ln1_b, w_q_mem, w_k_mem, w_v_mem, w_o_mem, ln2_g, ln2_b, ffn_w_up, ffn_conv_w, ffn_conv_b, ffn_w_down, ln3_g, ln3_b):
    weights = dict(
        w_in=w_in, g_q=g_q, g_kv=g_kv, w_uq=w_uq, w_uk=w_uk, w_uv=w_uv, rnn_conv_w=rnn_conv_w,
        rnn_conv_b=rnn_conv_b, rnn_wa=rnn_wa, rnn_ba=rnn_ba, rnn_wx=rnn_wx, rnn_bx=rnn_bx,
        rnn_lambda=rnn_lambda, g_attn_out=g_attn_out, g_rnn_out=g_rnn_out, w_out=w_out, ln1_g=ln1_g,
        ln1_b=ln1_b, w_q_mem=w_q_mem, w_k_mem=w_k_mem, w_v_mem=w_v_mem, w_o_mem=w_o_mem, ln2_g=ln2_g,
        ln2_b=ln2_b, ffn_w_up=ffn_w_up, ffn_conv_w=ffn_conv_w, ffn_conv_b=ffn_conv_b,
        ffn_w_down=ffn_w_down, ln3_g=ln3_g, ln3_b=ln3_b)
    depth = w_in.shape[0]
    assert depth == 1, "single-layer trunk"
    p = {name: w[0] for name, w in weights.items()}
    bp, sp, _ = x_prompt.shape
    bs, ts, _ = x_sample.shape
    past_len = page_table.shape[1] * PAGE_SIZE

    mem_rows = bp * N_MEM
    w_kv_mem = jnp.concatenate([p['w_k_mem'], p['w_v_mem']], axis=1).astype(BF16)
    mem_kv = pl.pallas_call(
        _matmul_kernel,
        grid=(mem_rows // TOKEN_TILE,),
        in_specs=[_rows(D_MODEL), _whole(w_kv_mem.shape)],
        out_specs=_rows(2 * D_MODEL),
        out_shape=jax.ShapeDtypeStruct((mem_rows, 2 * D_MODEL), F32),
        compiler_params=_params(1),
        name='mem_kv_prompt',
    )(mem_prompt.reshape(mem_rows, D_MODEL), w_kv_mem)
    mk = mem_kv[:, :D_MODEL].reshape(bp, N_MEM, MEM_HEADS, MEM_HEAD_DIM)
    mv = mem_kv[:, D_MODEL:].reshape(bp, N_MEM, MEM_HEADS, MEM_HEAD_DIM)

    yp, c_p, kr_p, rc_p, rh_p, fc_p = _group_forward(
        True, x_prompt, jnp.arange(sp, dtype=jnp.int32), p, mk, mv, None, None, None, None)

    h0_rows = jnp.pad(state_rnn_h[0][:, None, :], ((0, 0), (0, ts - 1), (0, 0))).reshape(bs * ts, D_RNN)
    ys, c_s, kr_s, rc_s, rh_s, fc_s = _group_forward(
        False, x_sample, past_len + jnp.arange(ts, dtype=jnp.int32), p, cache_mem_k[0], cache_mem_v[0],
        _seq_history(state_rnn_conv[0], ts), h0_rows, _seq_history(state_ffn_conv[0], ts),
        (page_table, cache_kv_latent[0], cache_k_rope[0]))

    lead = lambda t: t[None]
    return (yp, ys, lead(c_p), lead(kr_p), lead(rc_p), lead(rh_p), lead(mk), lead(mv), lead(fc_p),
            lead(c_s), lead(kr_s), lead(rc_s), lead(rh_s), lead(fc_s))
```
